```python
import jax, jax.numpy as jnp
from jax import lax
import numpy as np

D_MODEL = 1024
BATCH = 16
SEQ = 256
DEPTH = 1
DEC_BATCH = 8
DEC_SEQ = 2048
PAST_LEN = 256

GRID_W = 64
D_RNN = 1024
RNN_HEADS = 16
RNN_HB = D_RNN // RNN_HEADS
CONV_W = 4
RG_C = 8.0
D_POOL = 512
POOL_WINDOWS = (2, 4, 8, 16)
POOL_GROUPS = 4
POOL_GW = D_POOL // POOL_GROUPS
D_IN = 2 * D_RNN + D_POOL + 2 * D_MODEL
PEER_HEADS = 8
PEER_DKEY = 256
PEER_DHALF = PEER_DKEY // 2
N_KEYS = 128
N_EXPERTS = N_KEYS * N_KEYS
PEER_TOPK = 16
PEER_BLOCK = 128
EPS = 1e-6
POS_BASE = 10000.0

kernel_name = 'hybrid_rglru_pool_peer_flow_step'


def rmsnorm(x, g):
    xf = x.astype(jnp.float32)
    y = xf * lax.rsqrt(jnp.mean(xf * xf, axis=-1, keepdims=True) + EPS)
    return (y * g.astype(jnp.float32)).astype(x.dtype)


def grid_pos_embed(n_tokens, dtype):
    rows = n_tokens // GRID_W
    nq = D_MODEL // 4
    omega = 1.0 / (POS_BASE ** (jnp.arange(nq, dtype=jnp.float32) / nq))
    r = jnp.broadcast_to(jnp.arange(rows, dtype=jnp.float32)[:, None], (rows, GRID_W)).reshape(-1)
    col = jnp.broadcast_to(jnp.arange(GRID_W, dtype=jnp.float32)[None, :], (rows, GRID_W)).reshape(-1)
    er = r[:, None] * omega[None, :]
    ec = col[:, None] * omega[None, :]
    return jnp.concatenate([jnp.sin(er), jnp.cos(er), jnp.sin(ec), jnp.cos(ec)], axis=-1).astype(dtype)


def dwconv_centred(x, w, b):
    L = x.shape[1]
    xp = jnp.pad(x, ((0, 0), (CONV_W // 2, CONV_W - 1 - CONV_W // 2), (0, 0)))
    out = b
    for k in range(CONV_W):
        out = out + xp[:, k:k + L] * w[k]
    return out


def blockdiag(x, w, b):
    xh = x.reshape(x.shape[0], x.shape[1], RNN_HEADS, RNN_HB)
    return jnp.einsum('blhi,hij->blhj', xh, w).reshape(x.shape) + b


def linear_scan(a, b, h0):
    b = b.at[:, 0].add(a[:, 0] * h0)

    def combine(e1, e2):
        a1, b1 = e1
        a2, b2 = e2
        return a1 * a2, a2 * b1 + b2

    _, h = lax.associative_scan(combine, (a, b), axis=1)
    return h


def rglru_dir(x, wa, ba, wx, bx, lam, h0, reverse):
    xf = x.astype(jnp.float32)
    r = jax.nn.sigmoid(blockdiag(xf, wa.astype(jnp.float32), ba.astype(jnp.float32)))
    i = jax.nn.sigmoid(blockdiag(xf, wx.astype(jnp.float32), bx.astype(jnp.float32)))
    log_a = RG_C * r * jax.nn.log_sigmoid(lam.astype(jnp.float32))
    a = jnp.exp(log_a)
    bterm = jnp.sqrt(-jnp.expm1(2.0 * log_a)) * (i * xf)
    h0f = h0.astype(jnp.float32)
    if reverse:
        h = jnp.flip(linear_scan(jnp.flip(a, 1), jnp.flip(bterm, 1), h0f), 1)
        h_last = h[:, 0]
    else:
        h = linear_scan(a, bterm, h0f)
        h_last = h[:, -1]
    return h.astype(x.dtype), h_last.astype(x.dtype)


def pool_mix(x, pool_w, pool_b, pool_scale):
    B, L, _ = x.shape
    xf = x.astype(jnp.float32)
    cs = jnp.concatenate([jnp.zeros((B, 1, D_POOL), jnp.float32), jnp.cumsum(xf, axis=1)], axis=1)
    t = jnp.arange(L)
    outs = []
    for gi, w in enumerate(POOL_WINDOWS):
        lo = jnp.clip(t - w // 2, 0, L)
        hi = jnp.clip(t + w // 2, 0, L)
        csg = cs[..., gi * POOL_GW:(gi + 1) * POOL_GW]
        cnt = (hi - lo).astype(jnp.float32)[None, :, None]
        outs.append((csg[:, hi] - csg[:, lo]) / cnt)
    pooled = (jnp.concatenate(outs, axis=-1) - xf).astype(x.dtype)
    pg = pooled.reshape(B, L, POOL_GROUPS, POOL_GW)
    y = jnp.einsum('blgi,gij->blgj', pg, pool_w).reshape(B, L, D_POOL) + pool_b
    return y * pool_scale


def mixer(h, w_in, conv_w, conv_b, rg_wa, rg_ba, rg_wx, rg_bx, rg_lambda,
          w_up_rnn, pool_w, pool_b, pool_scale, w_up_pool, w_o, h0f, h0b):
    proj = h @ w_in
    s1, s2, s3, s4 = D_RNN, 2 * D_RNN, 2 * D_RNN + D_POOL, 2 * D_RNN + D_POOL + D_MODEL
    xr, yr, xp, ga, gb = proj[..., :s1], proj[..., s1:s2], proj[..., s2:s3], proj[..., s3:s4], proj[..., s4:]
    xr = dwconv_centred(xr, conv_w, conv_b)
    hf, sf = rglru_dir(xr, rg_wa[0], rg_ba[0], rg_wx[0], rg_bx[0], rg_lambda[0], h0f, False)
    hb, sb = rglru_dir(xr, rg_wa[1], rg_ba[1], rg_wx[1], rg_bx[1], rg_lambda[1], h0b, True)
    branch_a = ((hf + hb) * jax.nn.gelu(yr)) @ w_up_rnn
    branch_b = pool_mix(xp, pool_w, pool_b, pool_scale) @ w_up_pool
    merged = jax.nn.sigmoid(ga) * branch_a + jax.nn.sigmoid(gb) * branch_b
    return merged @ w_o, jnp.stack([sf, sb], axis=1)


def peer(h, wq, keys, u_tab, v_tab):
    B, L, D = h.shape
    T = B * L
    xt = h.reshape(T, D)
    q = (xt @ wq).reshape(T, PEER_HEADS, 2, PEER_DHALF)
    s = jnp.einsum('thpd,hpnd->thpn', q, keys).astype(jnp.float32)
    v1, i1 = lax.top_k(s[:, :, 0], PEER_TOPK)
    v2, i2 = lax.top_k(s[:, :, 1], PEER_TOPK)
    cand = (v1[..., :, None] + v2[..., None, :]).reshape(T, PEER_HEADS, PEER_TOPK * PEER_TOPK)
    sc, ci = lax.top_k(cand, PEER_TOPK)
    e = (jnp.take_along_axis(i1, ci // PEER_TOPK, axis=-1) * N_KEYS
         + jnp.take_along_axis(i2, ci % PEER_TOPK, axis=-1))
    g = jax.nn.softmax(sc, axis=-1).astype(h.dtype)
    nb = T // PEER_BLOCK
    hk = PEER_HEADS * PEER_TOPK

    def block(args):
        xb, eb, gbk = args
        ub = jnp.take(u_tab, eb, axis=0)
        act = jax.nn.gelu(jnp.einsum('cd,ced->ce', xb, ub))
        vb = jnp.take(v_tab, eb, axis=0)
        return jnp.einsum('ce,ced->cd', gbk * act, vb)

    out = lax.map(block, (xt.reshape(nb, PEER_BLOCK, D), e.reshape(nb, PEER_BLOCK, hk),
                          g.reshape(nb, PEER_BLOCK, hk)))
    return out.reshape(B, L, D)


def trunk_layer(x, mod, h0f, h0b, norm1_g, norm2_g, w_in, conv_w, conv_b, rg_wa, rg_ba, rg_wx, rg_bx,
                rg_lambda, w_up_rnn, pool_w, pool_b, pool_scale, w_up_pool, w_o,
                peer_wq, peer_keys, peer_u, peer_v):
    sh1, sc1, g1, sh2, sc2, g2 = jnp.split(mod.astype(x.dtype), 6, axis=-1)
    h = rmsnorm(x, norm1_g) * (1 + sc1) + sh1
    m, st = mixer(h, w_in, conv_w, conv_b, rg_wa, rg_ba, rg_wx, rg_bx, rg_lambda,
                  w_up_rnn, pool_w, pool_b, pool_scale, w_up_pool, w_o, h0f, h0b)
    x = x + g1 * m
    h = rmsnorm(x, norm2_g) * (1 + sc2) + sh2
    x = x + g2 * peer(h, peer_wq, peer_keys, peer_u, peer_v)
    return x, st


def setup_inputs(seed: int = 0) -> dict:
    key = jax.random.key(seed)
    ks = jax.random.split(key, 32)
    f32 = jnp.float32

    def nrm(k, shape, scale):
        return jax.random.normal(k, shape, f32) * scale

    a8 = jax.random.uniform(ks[14], (DEPTH, 2, D_RNN), f32, 0.9, 0.999)
    sig = a8 ** (1.0 / RG_C)
    rg_lambda = jnp.log(sig) - jnp.log1p(-sig)
    return {
        'x_prompt': nrm(ks[0], (BATCH, SEQ, D_MODEL), 1.0),
        'x_sample': nrm(ks[1], (DEC_BATCH, DEC_SEQ, D_MODEL), 1.0),
        'state_rglru': nrm(ks[2], (DEC_BATCH, DEPTH, 2, D_RNN), 0.5),
        'c': nrm(ks[3], (DEC_BATCH, D_MODEL), 1.0),
        'c_ctx': nrm(ks[4], (D_MODEL,), 1.0),
        'norm1_g': 1.0 + nrm(ks[5], (DEPTH, D_MODEL), 0.02),
        'norm2_g': 1.0 + nrm(ks[6], (DEPTH, D_MODEL), 0.02),
        'w_mod': nrm(ks[7], (DEPTH, D_MODEL, 6 * D_MODEL), D_MODEL ** -0.5),
        'b_mod': nrm(ks[8], (DEPTH, 6 * D_MODEL), 0.01),
        'w_in': nrm(ks[9], (DEPTH, D_MODEL, D_IN), D_MODEL ** -0.5),
        'conv_w': nrm(ks[10], (DEPTH, CONV_W, D_RNN), CONV_W ** -0.5),
        'conv_b': nrm(ks[11], (DEPTH, D_RNN), 0.01),
        'rg_wa': nrm(ks[12], (DEPTH, 2, RNN_HEADS, RNN_HB, RNN_HB), RNN_HB ** -0.5),
        'rg_ba': nrm(ks[13], (DEPTH, 2, D_RNN), 0.01),
        'rg_wx': nrm(ks[15], (DEPTH, 2, RNN_HEADS, RNN_HB, RNN_HB), RNN_HB ** -0.5),
        'rg_bx': nrm(ks[16], (DEPTH, 2, D_RNN), 0.01),
        'rg_lambda': rg_lambda,
        'w_up_rnn': nrm(ks[17], (DEPTH, D_RNN, D_MODEL), D_RNN ** -0.5),
        'pool_w': nrm(ks[18], (DEPTH, POOL_GROUPS, POOL_GW, POOL_GW), POOL_GW ** -0.5),
        'pool_b': nrm(ks[19], (DEPTH, D_POOL), 0.01),
        'pool_scale': 1.0 + nrm(ks[20], (DEPTH, D_POOL), 0.02),
        'w_up_pool': nrm(ks[21], (DEPTH, D_POOL, D_MODEL), D_POOL ** -0.5),
        'w_o': nrm(ks[22], (DEPTH, D_MODEL, D_MODEL), D_MODEL ** -0.5),
        'peer_wq': nrm(ks[23], (DEPTH, D_MODEL, PEER_HEADS * PEER_DKEY), D_MODEL ** -0.5),
        'peer_keys': nrm(ks[24], (DEPTH, PEER_HEADS, 2, N_KEYS, PEER_DHALF), PEER_DHALF ** -0.5),
        'peer_u': nrm(ks[25], (DEPTH, N_EXPERTS, D_MODEL), D_MODEL ** -0.5),
        'peer_v': nrm(ks[26], (DEPTH, N_EXPERTS, D_MODEL), 1.0),
        'final_g': 1.0 + nrm(ks[27], (D_MODEL,), 0.02),
    }


def reference(x_prompt, x_sample, state_rglru, c, c_ctx, norm1_g, norm2_g, w_mod, b_mod, w_in,
              conv_w, conv_b, rg_wa, rg_ba, rg_wx, rg_bx, rg_lambda, w_up_rnn, pool_w, pool_b,
              pool_scale, w_up_pool, w_o, peer_wq, peer_keys, peer_u, peer_v, final_g):
    xc = x_prompt
    xs = x_sample + grid_pos_embed(x_sample.shape[1], x_sample.dtype)[None]
    new_states = []
    for l in range(DEPTH):
        mod_ctx = (jax.nn.silu(c_ctx) @ w_mod[l] + b_mod[l])[None, None, :]
        mod_lat = (jax.nn.silu(c) @ w_mod[l] + b_mod[l])[:, None, :]
        h0 = jnp.zeros((xc.shape[0], D_RNN), xc.dtype)
        xc, st = trunk_layer(xc, mod_ctx, h0, h0, norm1_g[l], norm2_g[l], w_in[l], conv_w[l], conv_b[l],
                             rg_wa[l], rg_ba[l], rg_wx[l], rg_bx[l], rg_lambda[l], w_up_rnn[l], pool_w[l],
                             pool_b[l], pool_scale[l], w_up_pool[l], w_o[l], peer_wq[l], peer_keys[l],
                             peer_u[l], peer_v[l])
        new_states.append(st)
        xs, _ = trunk_layer(xs, mod_lat, state_rglru[:, l, 0], state_rglru[:, l, 1], norm1_g[l], norm2_g[l],
                            w_in[l], conv_w[l], conv_b[l], rg_wa[l], rg_ba[l], rg_wx[l], rg_bx[l],
                            rg_lambda[l], w_up_rnn[l], pool_w[l], pool_b[l], pool_scale[l], w_up_pool[l],
                            w_o[l], peer_wq[l], peer_keys[l], peer_u[l], peer_v[l])
    y_prompt = rmsnorm(xc, final_g)
    y_sample = rmsnorm(xs, final_g)
    new_state_rglru = jnp.stack(new_states, axis=1)
    return (y_prompt, y_sample, new_state_rglru)
```

```python
import functools

import jax
import jax.numpy as jnp
from jax import lax
from jax.experimental import pallas as pl
from jax.experimental.pallas import tpu as pltpu

F32 = jnp.float32
BF16 = jnp.bfloat16

D_MODEL = 1024
D_RNN = 1024
D_POOL = 512
POOL_WINDOWS = (2, 4, 8, 16)
POOL_GW = 128
RNN_HEADS = 16
RNN_HB = 64
RG_C = 8.0
EPS = 1e-6
GRID_W = 64
POS_BASE = 10000.0
PEER_HEADS = 8
N_KEYS = 128
N_EXPERTS = N_KEYS * N_KEYS
PEER_TOPK = 16

LANES = 128
SUBLANES = 8
VMEM_LIMIT = 56 * 1024 * 1024

TOK_TILE = 256
RNN_CB = 128
PEER_TM = 512
PEER_EB = 1024


def _cparams(sem):
    return pltpu.CompilerParams(dimension_semantics=sem, vmem_limit_bytes=VMEM_LIMIT)


def _const_spec(shape):
    nd = len(shape)
    return pl.BlockSpec(shape, lambda *_: (0,) * nd, pipeline_mode=pl.Buffered(1))


def _rmsnorm(x, g):
    ms = jnp.mean(x * x, axis=-1, keepdims=True)
    return x * lax.rsqrt(ms + EPS) * g


def _mod_kernel(c_ref, w_ref, b_ref, o_ref):
    c = c_ref[...]
    s = c * jax.nn.sigmoid(c)
    o_ref[...] = jnp.dot(s, w_ref[...], preferred_element_type=F32,
                         precision=lax.Precision.HIGHEST) + b_ref[...]


def _mod_call(cc, w_mod, b_mod):
    n = w_mod.shape[1]
    bn = 1536
    return pl.pallas_call(
        _mod_kernel,
        grid=(n // bn,),
        in_specs=[pl.BlockSpec(cc.shape, lambda j: (0, 0)),
                  pl.BlockSpec((D_MODEL, bn), lambda j: (0, j)),
                  pl.BlockSpec((1, bn), lambda j: (0, j))],
        out_specs=pl.BlockSpec((cc.shape[0], bn), lambda j: (0, j)),
        out_shape=jax.ShapeDtypeStruct((cc.shape[0], n), F32),
        compiler_params=_cparams(("arbitrary",)),
        name="mod",
    )(cc, w_mod, b_mod)


def _inproj_kernel(*refs, has_pos):
    if has_pos:
        x_ref, pos_ref, mod_ref, g_ref, w_ref, xr_ref, gy_ref, xp_ref, sga_ref, sgb_ref = refs
        x = x_ref[...] + pos_ref[...]
    else:
        x_ref, mod_ref, g_ref, w_ref, xr_ref, gy_ref, xp_ref, sga_ref, sgb_ref = refs
        x = x_ref[...]
    mod = mod_ref[0]
    sh1 = mod[:, 0:D_MODEL]
    sc1 = mod[:, D_MODEL:2 * D_MODEL]
    h = _rmsnorm(x, g_ref[...]) * (1.0 + sc1) + sh1
    hb = h.astype(BF16)
    s1, s2, s3, s4 = D_RNN, 2 * D_RNN, 2 * D_RNN + D_POOL, 2 * D_RNN + D_POOL + D_MODEL
    xr_ref[...] = jnp.dot(hb, w_ref[:, 0:s1], preferred_element_type=F32)
    gy_ref[...] = jax.nn.gelu(jnp.dot(hb, w_ref[:, s1:s2], preferred_element_type=F32))
    xp_ref[...] = jnp.dot(hb, w_ref[:, s2:s3], preferred_element_type=F32)
    sga_ref[...] = jax.nn.sigmoid(jnp.dot(hb, w_ref[:, s3:s4], preferred_element_type=F32))
    sgb_ref[...] = jax.nn.sigmoid(jnp.dot(hb, w_ref[:, s4:], preferred_element_type=F32))


def _tile_maps(n_tiles, n_mod, tiles_per_seq):
    tiles_per_mod = n_tiles // n_mod
    tok = lambda i: (i, 0)
    modm = lambda i: (i // tiles_per_mod, 0, 0)
    posm = lambda i: (i % tiles_per_seq, 0)
    return tok, modm, posm


def _inproj_call(x, pos, mods, norm_g, w_in_bf, seq_len):
    t = x.shape[0]
    tm = TOK_TILE
    n_tiles = t // tm
    tok, modm, posm = _tile_maps(n_tiles, mods.shape[0], seq_len // tm)
    d_in = w_in_bf.shape[1]
    in_specs = [pl.BlockSpec((tm, D_MODEL), tok)]
    args = [x]
    if pos is not None:
        in_specs.append(pl.BlockSpec((tm, D_MODEL), posm))
        args.append(pos)
    in_specs += [pl.BlockSpec((1, 1, mods.shape[2]), modm),
                 _const_spec((1, D_MODEL)),
                 _const_spec((D_MODEL, d_in))]
    args += [mods, norm_g, w_in_bf]
    widths = (D_RNN, D_RNN, D_POOL, D_MODEL, D_MODEL)
    return pl.pallas_call(
        functools.partial(_inproj_kernel, has_pos=pos is not None),
        grid=(n_tiles,),
        in_specs=in_specs,
        out_specs=[pl.BlockSpec((tm, w), tok) for w in widths],
        out_shape=[jax.ShapeDtypeStruct((t, w), F32) for w in widths],
        compiler_params=_cparams(("parallel",)),
        name="inproj",
    )(*args)


def _rglru_kernel(xr_ref, cw_ref, cb_ref, wa_ref, wx_ref, ba_ref, bx_ref, lam_ref, h0_ref,
                  hs_ref, st_ref, af_ref, bf_ref, ab_ref, bb_ref, *, seq_len):
    L = seq_len
    x = xr_ref[0]
    C = x.shape[1]
    row = lax.broadcasted_iota(jnp.int32, (L, C), 0)

    def shifted(v, s):
        r = pltpu.roll(v, s % L, 0)
        return jnp.where(row >= s, r, 0.0) if s > 0 else jnp.where(row < L + s, r, 0.0)

    cw = cw_ref[...]
    xc = (cb_ref[...] + cw[0:1] * shifted(x, 2) + cw[1:2] * shifted(x, 1)
          + cw[2:3] * x + cw[3:4] * shifted(x, -1))
    xcb = xc.astype(BF16)
    r8 = row & (SUBLANES - 1)

    for d, (a_ref, b_ref) in enumerate(((af_ref, bf_ref), (ab_ref, bb_ref))):
        r = jax.nn.sigmoid(jnp.dot(xcb, wa_ref[d, 0], preferred_element_type=F32) + ba_ref[d:d + 1])
        i = jax.nn.sigmoid(jnp.dot(xcb, wx_ref[d, 0], preferred_element_type=F32) + bx_ref[d:d + 1])
        lam = lam_ref[d:d + 1]
        log_sig = jnp.minimum(lam, 0.0) - jnp.log1p(jnp.exp(-jnp.abs(lam)))
        log_a = RG_C * r * log_sig
        a = jnp.exp(log_a)
        th = jnp.tanh(log_a)
        b = jnp.sqrt(-2.0 * th / (1.0 - th)) * (i * xc)
        for s in (1, 2, 4):
            if d == 0:
                a_sh, b_sh = pltpu.roll(a, s, 0), pltpu.roll(b, s, 0)
                valid = r8 >= s
            else:
                a_sh, b_sh = pltpu.roll(a, L - s, 0), pltpu.roll(b, L - s, 0)
                valid = r8 < SUBLANES - s
            b = jnp.where(valid, a * b_sh + b, b)
            a = jnp.where(valid, a * a_sh, a)
        a_ref[...] = a
        b_ref[...] = b

    nch = L // SUBLANES
    h0 = h0_ref[0]

    def step(c, carry, accumulate):
        hf, hb = carry
        of = pl.multiple_of(c * SUBLANES, SUBLANES)
        ob = pl.multiple_of((nch - 1 - c) * SUBLANES, SUBLANES)
        blk_f = af_ref[pl.ds(of, SUBLANES), :] * hf + bf_ref[pl.ds(of, SUBLANES), :]
        blk_b = ab_ref[pl.ds(ob, SUBLANES), :] * hb + bb_ref[pl.ds(ob, SUBLANES), :]
        if accumulate:
            hs_ref[0, pl.ds(of, SUBLANES), :] += blk_f
            hs_ref[0, pl.ds(ob, SUBLANES), :] += blk_b
        else:
            hs_ref[0, pl.ds(of, SUBLANES), :] = blk_f
            hs_ref[0, pl.ds(ob, SUBLANES), :] = blk_b
        return blk_f[SUBLANES - 1:SUBLANES], blk_b[0:1]

    carry = lax.fori_loop(0, nch // 2, functools.partial(step, accumulate=False),
                          (h0[0:1], h0[1:2]))
    hf, hb = lax.fori_loop(nch // 2, nch, functools.partial(step, accumulate=True), carry)
    st_ref[0] = jnp.concatenate([hf, hb], axis=0)


def _rglru_call(xr, conv_w, conv_b, wa_bd, wx_bd, ba, bx, lam, h0, batch, seq_len):
    cb = RNN_CB
    ncb = D_RNN // cb
    xr3 = xr.reshape(batch, seq_len, D_RNN)
    hs, st = pl.pallas_call(
        functools.partial(_rglru_kernel, seq_len=seq_len),
        grid=(batch, ncb),
        in_specs=[pl.BlockSpec((1, seq_len, cb), lambda b, c: (b, 0, c)),
                  pl.BlockSpec((4, cb), lambda b, c: (0, c)),
                  pl.BlockSpec((1, cb), lambda b, c: (0, c)),
                  pl.BlockSpec((2, 1, cb, cb), lambda b, c: (0, c, 0, 0)),
                  pl.BlockSpec((2, 1, cb, cb), lambda b, c: (0, c, 0, 0)),
                  pl.BlockSpec((2, cb), lambda b, c: (0, c)),
                  pl.BlockSpec((2, cb), lambda b, c: (0, c)),
                  pl.BlockSpec((2, cb), lambda b, c: (0, c)),
                  pl.BlockSpec((1, 2, cb), lambda b, c: (b, 0, c))],
        out_specs=[pl.BlockSpec((1, seq_len, cb), lambda b, c: (b, 0, c)),
                   pl.BlockSpec((1, 2, cb), lambda b, c: (b, 0, c))],
        out_shape=[jax.ShapeDtypeStruct((batch, seq_len, D_RNN), F32),
                   jax.ShapeDtypeStruct((batch, 2, D_RNN), F32)],
        scratch_shapes=[pltpu.VMEM((seq_len, cb), F32) for _ in range(4)],
        compiler_params=_cparams(("parallel", "parallel")),
        name="rglru",
    )(xr3, conv_w, conv_b, wa_bd, wx_bd, ba, bx, lam, h0)
    return hs.reshape(batch * seq_len, D_RNN), st


def _pool_kernel(xp_ref, w_ref, b_ref, sc_ref, o_ref, *, seq_len):
    L = seq_len
    row = lax.broadcasted_iota(jnp.int32, (L, POOL_GW), 0)

    def shifted(v, s):
        r = pltpu.roll(v, s % L, 0)
        return jnp.where(row >= s, r, 0.0) if s > 0 else jnp.where(row < L + s, r, 0.0)

    for g, w in enumerate(POOL_WINDOWS):
        lanes = slice(g * POOL_GW, (g + 1) * POOL_GW)
        x = xp_ref[0, :, lanes]
        half = w // 2
        acc = x
        for k in range(-half, half):
            if k != 0:
                acc = acc + shifted(x, -k)
        cnt = (jnp.minimum(row + half, L) - jnp.maximum(row - half, 0)).astype(F32)
        pooled = acc / cnt - x
        y = jnp.dot(pooled.astype(BF16), w_ref[g], preferred_element_type=F32) + b_ref[:, lanes]
        o_ref[0, :, lanes] = y * sc_ref[:, lanes]


def _pool_call(xp, pool_w_bf, pool_b, pool_scale, batch, seq_len):
    xp3 = xp.reshape(batch, seq_len, D_POOL)
    out = pl.pallas_call(
        functools.partial(_pool_kernel, seq_len=seq_len),
        grid=(batch,),
        in_specs=[pl.BlockSpec((1, seq_len, D_POOL), lambda b: (b, 0, 0)),
                  pl.BlockSpec(pool_w_bf.shape, lambda b: (0, 0, 0)),
                  pl.BlockSpec((1, D_POOL), lambda b: (0, 0)),
                  pl.BlockSpec((1, D_POOL), lambda b: (0, 0))],
        out_specs=pl.BlockSpec((1, seq_len, D_POOL), lambda b: (b, 0, 0)),
        out_shape=jax.ShapeDtypeStruct((batch, seq_len, D_POOL), F32),
        compiler_params=_cparams(("parallel",)),
        name="pool",
    )(xp3, pool_w_bf, pool_b, pool_scale)
    return out.reshape(batch * seq_len, D_POOL)


def _mixout_kernel(*refs, has_pos):
    if has_pos:
        (hs_ref, gy_ref, yp_ref, sga_ref, sgb_ref, x_ref, pos_ref, mod_ref, wur_ref, wup_ref,
         wo_ref, g_ref, x1_ref, h2_ref) = refs
        x = x_ref[...] + pos_ref[...]
    else:
        (hs_ref, gy_ref, yp_ref, sga_ref, sgb_ref, x_ref, mod_ref, wur_ref, wup_ref,
         wo_ref, g_ref, x1_ref, h2_ref) = refs
        x = x_ref[...]
    mod = mod_ref[0]
    g1 = mod[:, 2 * D_MODEL:3 * D_MODEL]
    sh2 = mod[:, 3 * D_MODEL:4 * D_MODEL]
    sc2 = mod[:, 4 * D_MODEL:5 * D_MODEL]
    branch_a = jnp.dot((hs_ref[...] * gy_ref[...]).astype(BF16), wur_ref[...],
                       preferred_element_type=F32)
    branch_b = jnp.dot(yp_ref[...].astype(BF16), wup_ref[...], preferred_element_type=F32)
    merged = sga_ref[...] * branch_a + sgb_ref[...] * branch_b
    m = jnp.dot(merged.astype(BF16), wo_ref[...], preferred_element_type=F32)
    x1 = x + g1 * m
    x1_ref[...] = x1
    h2_ref[...] = (_rmsnorm(x1, g_ref[...]) * (1.0 + sc2) + sh2).astype(BF16)


def _mixout_call(hs, gy, yp, sga, sgb, x, pos, mods, wur_bf, wup_bf, wo_bf, norm_g, seq_len):
    t = x.shape[0]
    tm = TOK_TILE
    n_tiles = t // tm
    tok, modm, posm = _tile_maps(n_tiles, mods.shape[0], seq_len // tm)
    in_specs = [pl.BlockSpec((tm, D_RNN), tok), pl.BlockSpec((tm, D_RNN), tok),
                pl.BlockSpec((tm, D_POOL), tok), pl.BlockSpec((tm, D_MODEL), tok),
                pl.BlockSpec((tm, D_MODEL), tok), pl.BlockSpec((tm, D_MODEL), tok)]
    args = [hs, gy, yp, sga, sgb, x]
    if pos is not None:
        in_specs.append(pl.BlockSpec((tm, D_MODEL), posm))
        args.append(pos)
    in_specs += [pl.BlockSpec((1, 1, mods.shape[2]), modm),
                 _const_spec((D_RNN, D_MODEL)), _const_spec((D_POOL, D_MODEL)),
                 _const_spec((D_MODEL, D_MODEL)), _const_spec((1, D_MODEL))]
    args += [mods, wur_bf, wup_bf, wo_bf, norm_g]
    return pl.pallas_call(
        functools.partial(_mixout_kernel, has_pos=pos is not None),
        grid=(n_tiles,),
        in_specs=in_specs,
        out_specs=[pl.BlockSpec((tm, D_MODEL), tok), pl.BlockSpec((tm, D_MODEL), tok)],
        out_shape=[jax.ShapeDtypeStruct((t, D_MODEL), F32),
                   jax.ShapeDtypeStruct((t, D_MODEL), BF16)],
        compiler_params=_cparams(("parallel",)),
        name="mixout",
    )(*args)


def _sort16_pairs():
    pairs = []

    def merge(lo, n, r):
        step = r * 2
        if step < n:
            merge(lo, n, step)
            merge(lo + r, n, step)
            pairs.extend((i, i + r) for i in range(lo + r, lo + n - r, step))
        else:
            pairs.append((lo, lo + r))

    def sort(lo, n):
        if n > 1:
            sort(lo, n // 2)
            sort(lo + n // 2, n // 2)
            merge(lo, n, 1)

    sort(0, PEER_TOPK)
    return pairs


_SORT16 = _sort16_pairs()


def _cmpx(v, i, j):
    hi, lo = jnp.maximum(v[i], v[j]), jnp.minimum(v[i], v[j])
    v[i], v[j] = hi, lo


def _bitonic_merge16(v):
    for d in (8, 4, 2, 1):
        for i in range(PEER_TOPK):
            if not i & d:
                _cmpx(v, i, i + d)
    return v


def _merge_top16(top, rest, other, other_rest=None):
    n = len(other)
    z = list(top)
    for k in range(n):
        i = PEER_TOPK - 1 - k
        z[i] = jnp.maximum(top[i], other[k])
        rest = jnp.maximum(rest, jnp.minimum(top[i], other[k]))
    if other_rest is not None:
        rest = jnp.maximum(rest, other_rest)
    return _bitonic_merge16(z), rest


def _top17_sublane_keys(s):
    v = [s[k * SUBLANES:(k + 1) * SUBLANES, :] for k in range(N_KEYS // SUBLANES)]
    for i, j in _SORT16:
        _cmpx(v, i, j)
    rest = jnp.full_like(v[0], -jnp.inf)
    for sh in (4, 2, 1):
        other = [pltpu.roll(x, sh, 0) for x in v]
        v, rest = _merge_top16(v, rest, other, pltpu.roll(rest, sh, 0))
    return v, rest


def _peer_route(h2_ref, wqt_ref, keys_ref, s2_ref, e2_ref, th_ref, f1_ref):
    tm = h2_ref.shape[0]
    qt = lax.dot_general(wqt_ref[...], h2_ref[...], (((1,), (1,)), ((), ())),
                         preferred_element_type=F32)
    sub = lax.broadcasted_iota(jnp.int32, (SUBLANES, tm), 0)
    n = PEER_TOPK + 1
    packed = [[None] * n, [None] * n]
    for h in range(PEER_HEADS):
        for p in range(2):
            r0 = (h * 2 + p) * N_KEYS
            s = jnp.dot(keys_ref[h * 2 + p], qt[r0:r0 + N_KEYS, :].astype(BF16),
                        preferred_element_type=F32)
            (th_ref if p == 0 else s2_ref)[h] = s
            v, rest = _top17_sublane_keys(s)
            v = v + [rest]
            for k in range(n):
                packed[p][k] = v[k] if h == 0 else jnp.where(sub == h, v[k], packed[p][k])
    v1, v2 = packed
    top = [v1[0] + v2[j] for j in range(PEER_TOPK)]
    rest = jnp.full_like(top[0], -jnp.inf)
    for i in range(1, PEER_TOPK):
        row = [v1[i] + v2[j] for j in range(PEER_TOPK // (i + 1))]
        top, rest = _merge_top16(top, rest, row)
    e1, e2 = v1[PEER_TOPK] + v2[0], v1[0] + v2[PEER_TOPK]
    top, rest = _merge_top16(top, rest, [jnp.maximum(e1, e2), jnp.minimum(e1, e2)])
    z = jnp.zeros_like(top[0])
    for k in range(PEER_TOPK):
        z = z + jnp.exp(top[k] - top[0])
    inv_z = 1.0 / z
    tau = 0.5 * (top[PEER_TOPK - 1] + rest)
    for h in range(PEER_HEADS):
        s1 = th_ref[h]
        th_ref[h] = tau[h:h + 1] - s1
        f1_ref[h] = jnp.exp(s1 - v1[0][h:h + 1]) * inv_z[h:h + 1]
        e2_ref[h] = jnp.exp(s2_ref[h] - v2[0][h:h + 1])


def _peer_kernel(h2_ref, x1_ref, mod_ref, wqt_ref, keys_ref, u_ref, vt_ref, g_ref, y_ref,
                 s2_ref, e2_ref, th_ref, f1_ref, acc_ref, pt_ref):
    j = pl.program_id(1)
    tm = h2_ref.shape[0]
    eb = u_ref.shape[0]

    @pl.when(j == 0)
    def _():
        _peer_route(h2_ref, wqt_ref, keys_ref, s2_ref, e2_ref, th_ref, f1_ref)
        acc_ref[...] = jnp.zeros_like(acc_ref)

    st = lax.dot_general(u_ref[...], h2_ref[...], (((1,), (1,)), ((), ())),
                         preferred_element_type=F32)
    act = jax.nn.gelu(st)
    assert eb // N_KEYS == SUBLANES
    rows8 = pl.ds(pl.multiple_of(j * SUBLANES, SUBLANES), SUBLANES)
    for a in range(eb // N_KEYS):
        for tc in range(tm // LANES):
            cols = pl.ds(tc * LANES, LANES)
            w = jnp.zeros((N_KEYS, LANES), F32)
            for h in range(PEER_HEADS):
                th_row = th_ref[h, rows8, cols][a:a + 1]
                f1_row = f1_ref[h, rows8, cols][a:a + 1]
                w = w + jnp.where(s2_ref[h, :, cols] >= th_row, e2_ref[h, :, cols] * f1_row, 0.0)
            tile = act[a * N_KEYS:(a + 1) * N_KEYS, tc * LANES:(tc + 1) * LANES] * w
            pt_ref[a * N_KEYS:(a + 1) * N_KEYS, cols] = tile.astype(BF16)
    acc_ref[...] += jnp.dot(vt_ref[...], pt_ref[...], preferred_element_type=F32)

    @pl.when(j == pl.num_programs(1) - 1)
    def _():
        g2 = mod_ref[0][:, 5 * D_MODEL:6 * D_MODEL]
        xo = x1_ref[...] + g2 * acc_ref[...].T
        y_ref[...] = _rmsnorm(xo, g_ref[...])


def _peer_call(h2, x1, mods, wqt_bf, keys_bf, u_bf, vt_bf, final_g, seq_len):
    t = h2.shape[0]
    tm, eb = PEER_TM, PEER_EB
    n_tiles = t // tm
    tiles_per_mod = n_tiles // mods.shape[0]
    tok = lambda i, j: (i, 0)
    return pl.pallas_call(
        _peer_kernel,
        grid=(n_tiles, N_EXPERTS // eb),
        in_specs=[pl.BlockSpec((tm, D_MODEL), tok),
                  pl.BlockSpec((tm, D_MODEL), tok),
                  pl.BlockSpec((1, 1, mods.shape[2]), lambda i, j: (i // tiles_per_mod, 0, 0)),
                  pl.BlockSpec(wqt_bf.shape, lambda i, j: (0, 0), pipeline_mode=pl.Buffered(1)),
                  pl.BlockSpec(keys_bf.shape, lambda i, j: (0, 0, 0), pipeline_mode=pl.Buffered(1)),
                  pl.BlockSpec((eb, D_MODEL), lambda i, j: (j, 0)),
                  pl.BlockSpec((D_MODEL, eb), lambda i, j: (0, j)),
                  pl.BlockSpec((1, D_MODEL), lambda i, j: (0, 0), pipeline_mode=pl.Buffered(1))],
        out_specs=pl.BlockSpec((tm, D_MODEL), tok),
        out_shape=jax.ShapeDtypeStruct((t, D_MODEL), F32),
        scratch_shapes=[pltpu.VMEM((PEER_HEADS, N_KEYS, tm), F32) for _ in range(4)]
        + [pltpu.VMEM((D_MODEL, tm), F32), pltpu.VMEM((eb, tm), BF16)],
        compiler_params=_cparams(("parallel", "arbitrary")),
        name="peer",
    )(h2, x1, mods, wqt_bf, keys_bf, u_bf, vt_bf, final_g)


def _grid_pos_embed(n_tokens):
    rows = n_tokens // GRID_W
    nq = D_MODEL // 4
    omega = 1.0 / (POS_BASE ** (jnp.arange(nq, dtype=F32) / nq))
    r = jnp.broadcast_to(jnp.arange(rows, dtype=F32)[:, None], (rows, GRID_W)).reshape(-1)
    col = jnp.broadcast_to(jnp.arange(GRID_W, dtype=F32)[None, :], (rows, GRID_W)).reshape(-1)
    er = r[:, None] * omega[None, :]
    ec = col[:, None] * omega[None, :]
    return jnp.concatenate([jnp.sin(er), jnp.cos(er), jnp.sin(ec), jnp.cos(ec)], axis=-1)


def _block_diag(w):
    per = RNN_CB // RNN_HB
    w5 = w.reshape(2, D_RNN // RNN_CB, per, RNN_HB, RNN_HB)
    eye = jnp.eye(per, dtype=w.dtype)
    bd = jnp.einsum("dcipq,ij->dcipjq", w5, eye)
    return bd.reshape(2, D_RNN // RNN_CB, RNN_CB, RNN_CB).astype(BF16)


def kernel(x_prompt, x_sample, state_rglru, c, c_ctx, norm1_g, norm2_g, w_mod, b_mod, w_in, conv_w,
           conv_b, rg_wa, rg_ba, rg_wx, rg_bx, rg_lambda, w_up_rnn, pool_w, pool_b, pool_scale,
           w_up_pool, w_o, peer_wq, peer_keys, peer_u, peer_v, final_g):
    assert w_mod.shape[0] == 1, "single trunk layer"
    l = 0
    batch, seq = x_prompt.shape[0], x_prompt.shape[1]
    dec_batch, dec_seq = x_sample.shape[0], x_sample.shape[1]

    n_c = 1 + dec_batch
    cc = jnp.zeros((2 * SUBLANES, D_MODEL), F32).at[0].set(c_ctx).at[1:n_c].set(c)
    mods = _mod_call(cc, w_mod[l], b_mod[l][None])
    mods_ctx = mods[0:1][:, None, :]
    mods_lat = mods[1:n_c][:, None, :]

    w_in_bf = w_in[l].astype(BF16)
    wa_bd, wx_bd = _block_diag(rg_wa[l]), _block_diag(rg_wx[l])
    pool_w_bf = pool_w[l].astype(BF16)
    wur_bf, wup_bf, wo_bf = w_up_rnn[l].astype(BF16), w_up_pool[l].astype(BF16), w_o[l].astype(BF16)
    wqt_bf = peer_wq[l].T.astype(BF16)
    keys_bf = peer_keys[l].reshape(PEER_HEADS * 2, N_KEYS, N_KEYS).astype(BF16)
    u_bf = peer_u[l].astype(BF16)
    vt_bf = peer_v[l].T.astype(BF16)
    n1, n2, fg = norm1_g[l][None], norm2_g[l][None], final_g[None]
    pos = _grid_pos_embed(dec_seq)

    def stream(x, pos_emb, mods_s, h0, b, seq_len):
        xr, gy, xp, sga, sgb = _inproj_call(x, pos_emb, mods_s, n1, w_in_bf, seq_len)
        hs, st = _rglru_call(xr, conv_w[l], conv_b[l][None], wa_bd, wx_bd, rg_ba[l], rg_bx[l],
                             rg_lambda[l], h0, b, seq_len)
        yp = _pool_call(xp, pool_w_bf, pool_b[l][None], pool_scale[l][None], b, seq_len)
        x1, h2 = _mixout_call(hs, gy, yp, sga, sgb, x, pos_emb, mods_s, wur_bf, wup_bf, wo_bf, n2,
                              seq_len)
        y = _peer_call(h2, x1, mods_s, wqt_bf, keys_bf, u_bf, vt_bf, fg, seq_len)
        return y.reshape(b, seq_len, D_MODEL), st

    h0_ctx = jnp.zeros((batch, 2, D_RNN), F32)
    y_prompt, st_ctx = stream(x_prompt.reshape(batch * seq, D_MODEL), None, mods_ctx, h0_ctx,
                              batch, seq)
    y_sample, _ = stream(x_sample.reshape(dec_batch * dec_seq, D_MODEL), pos, mods_lat,
                         state_rglru[:, l], dec_batch, dec_seq)
    return (y_prompt, y_sample, st_ctx[:, None])
```

```python
import functools

import jax
import jax.numpy as jnp
from jax import lax
from jax.experimental import pallas as pl
from jax.experimental.pallas import tpu as pltpu

F32 = jnp.float32
BF16 = jnp.bfloat16

D_MODEL = 1024
D_RNN = 1024
D_POOL = 512
POOL_WINDOWS = (2, 4, 8, 16)
POOL_GW = 128
RNN_HEADS = 16
RNN_HB = 64
RG_C = 8.0
EPS = 1e-6
GRID_W = 64
POS_BASE = 10000.0
PEER_HEADS = 8
N_KEYS = 128
N_EXPERTS = N_KEYS * N_KEYS
PEER_TOPK = 16

LANES = 128
SUBLANES = 8
VMEM_LIMIT = 56 * 1024 * 1024

TOK_TILE = 256
RNN_CB = 128
PEER_TM = 512
PEER_EB = 512
PEER_SCHED_FLAGS = None


def _cparams(sem, flags=None):
    return pltpu.CompilerParams(dimension_semantics=sem, vmem_limit_bytes=VMEM_LIMIT, flags=flags)


def _const_spec(shape):
    nd = len(shape)
    return pl.BlockSpec(shape, lambda *_: (0,) * nd, pipeline_mode=pl.Buffered(1))


def _rmsnorm(x, g):
    ms = jnp.mean(x * x, axis=-1, keepdims=True)
    return x * lax.rsqrt(ms + EPS) * g


def _mod_kernel(c_ref, w_ref, b_ref, o_ref):
    c = c_ref[...]
    s = c * jax.nn.sigmoid(c)
    o_ref[...] = jnp.dot(s, w_ref[...], preferred_element_type=F32,
                         precision=lax.Precision.HIGHEST) + b_ref[...]


def _mod_call(cc, w_mod, b_mod):
    n = w_mod.shape[1]
    bn = 1536
    return pl.pallas_call(
        _mod_kernel,
        grid=(n // bn,),
        in_specs=[pl.BlockSpec(cc.shape, lambda j: (0, 0)),
                  pl.BlockSpec((D_MODEL, bn), lambda j: (0, j)),
                  pl.BlockSpec((1, bn), lambda j: (0, j))],
        out_specs=pl.BlockSpec((cc.shape[0], bn), lambda j: (0, j)),
        out_shape=jax.ShapeDtypeStruct((cc.shape[0], n), F32),
        compiler_params=_cparams(("arbitrary",)),
        name="mod",
    )(cc, w_mod, b_mod)


def _inproj_kernel(*refs, has_pos):
    if has_pos:
        x_ref, pos_ref, mod_ref, g_ref, w_ref, xr_ref, gy_ref, xp_ref, sga_ref, sgb_ref = refs
        x = x_ref[...] + pos_ref[...]
    else:
        x_ref, mod_ref, g_ref, w_ref, xr_ref, gy_ref, xp_ref, sga_ref, sgb_ref = refs
        x = x_ref[...]
    mod = mod_ref[0]
    sh1 = mod[:, 0:D_MODEL]
    sc1 = mod[:, D_MODEL:2 * D_MODEL]
    h = _rmsnorm(x, g_ref[...]) * (1.0 + sc1) + sh1
    hb = h.astype(BF16)
    s1, s2, s3, s4 = D_RNN, 2 * D_RNN, 2 * D_RNN + D_POOL, 2 * D_RNN + D_POOL + D_MODEL
    xr_ref[...] = jnp.dot(hb, w_ref[:, 0:s1], preferred_element_type=F32)
    gy_ref[...] = jax.nn.gelu(jnp.dot(hb, w_ref[:, s1:s2], preferred_element_type=F32))
    xp_ref[...] = jnp.dot(hb, w_ref[:, s2:s3], preferred_element_type=F32)
    sga_ref[...] = jax.nn.sigmoid(jnp.dot(hb, w_ref[:, s3:s4], preferred_element_type=F32))
    sgb_ref[...] = jax.nn.sigmoid(jnp.dot(hb, w_ref[:, s4:], preferred_element_type=F32))


def _tile_maps(n_tiles, n_mod, tiles_per_seq):
    tiles_per_mod = n_tiles // n_mod
    tok = lambda i: (i, 0)
    modm = lambda i: (i // tiles_per_mod, 0, 0)
    posm = lambda i: (i % tiles_per_seq, 0)
    return tok, modm, posm


def _inproj_call(x, pos, mods, norm_g, w_in_bf, seq_len):
    t = x.shape[0]
    tm = TOK_TILE
    n_tiles = t // tm
    tok, modm, posm = _tile_maps(n_tiles, mods.shape[0], seq_len // tm)
    d_in = w_in_bf.shape[1]
    in_specs = [pl.BlockSpec((tm, D_MODEL), tok)]
    args = [x]
    if pos is not None:
        in_specs.append(pl.BlockSpec((tm, D_MODEL), posm))
        args.append(pos)
    in_specs += [pl.BlockSpec((1, 1, mods.shape[2]), modm),
                 _const_spec((1, D_MODEL)),
                 _const_spec((D_MODEL, d_in))]
    args += [mods, norm_g, w_in_bf]
    widths = (D_RNN, D_RNN, D_POOL, D_MODEL, D_MODEL)
    return pl.pallas_call(
        functools.partial(_inproj_kernel, has_pos=pos is not None),
        grid=(n_tiles,),
        in_specs=in_specs,
        out_specs=[pl.BlockSpec((tm, w), tok) for w in widths],
        out_shape=[jax.ShapeDtypeStruct((t, w), F32) for w in widths],
        compiler_params=_cparams(("parallel",)),
        name="inproj",
    )(*args)


def _rglru_kernel(xr_ref, cw_ref, cb_ref, wa_ref, wx_ref, ba_ref, bx_ref, lam_ref, h0_ref,
                  hs_ref, st_ref, af_ref, bf_ref, ab_ref, bb_ref, *, seq_len):
    L = seq_len
    x = xr_ref[0]
    C = x.shape[1]
    row = lax.broadcasted_iota(jnp.int32, (L, C), 0)

    def shifted(v, s):
        r = pltpu.roll(v, s % L, 0)
        return jnp.where(row >= s, r, 0.0) if s > 0 else jnp.where(row < L + s, r, 0.0)

    cw = cw_ref[...]
    xc = (cb_ref[...] + cw[0:1] * shifted(x, 2) + cw[1:2] * shifted(x, 1)
          + cw[2:3] * x + cw[3:4] * shifted(x, -1))
    xcb = xc.astype(BF16)
    r8 = row & (SUBLANES - 1)

    for d, (a_ref, b_ref) in enumerate(((af_ref, bf_ref), (ab_ref, bb_ref))):
        r = jax.nn.sigmoid(jnp.dot(xcb, wa_ref[d, 0], preferred_element_type=F32) + ba_ref[d:d + 1])
        i = jax.nn.sigmoid(jnp.dot(xcb, wx_ref[d, 0], preferred_element_type=F32) + bx_ref[d:d + 1])
        lam = lam_ref[d:d + 1]
        log_sig = jnp.minimum(lam, 0.0) - jnp.log1p(jnp.exp(-jnp.abs(lam)))
        log_a = RG_C * r * log_sig
        a = jnp.exp(log_a)
        th = jnp.tanh(log_a)
        b = jnp.sqrt(-2.0 * th / (1.0 - th)) * (i * xc)
        for s in (1, 2, 4):
            if d == 0:
                a_sh, b_sh = pltpu.roll(a, s, 0), pltpu.roll(b, s, 0)
                valid = r8 >= s
            else:
                a_sh, b_sh = pltpu.roll(a, L - s, 0), pltpu.roll(b, L - s, 0)
                valid = r8 < SUBLANES - s
            b = jnp.where(valid, a * b_sh + b, b)
            a = jnp.where(valid, a * a_sh, a)
        a_ref[...] = a
        b_ref[...] = b

    nch = L // SUBLANES
    h0 = h0_ref[0]

    def step(c, carry, accumulate):
        hf, hb = carry
        of = pl.multiple_of(c * SUBLANES, SUBLANES)
        ob = pl.multiple_of((nch - 1 - c) * SUBLANES, SUBLANES)
        blk_f = af_ref[pl.ds(of, SUBLANES), :] * hf + bf_ref[pl.ds(of, SUBLANES), :]
        blk_b = ab_ref[pl.ds(ob, SUBLANES), :] * hb + bb_ref[pl.ds(ob, SUBLANES), :]
        if accumulate:
            hs_ref[0, pl.ds(of, SUBLANES), :] += blk_f
            hs_ref[0, pl.ds(ob, SUBLANES), :] += blk_b
        else:
            hs_ref[0, pl.ds(of, SUBLANES), :] = blk_f
            hs_ref[0, pl.ds(ob, SUBLANES), :] = blk_b
        return blk_f[SUBLANES - 1:SUBLANES], blk_b[0:1]

    carry = lax.fori_loop(0, nch // 2, functools.partial(step, accumulate=False),
                          (h0[0:1], h0[1:2]))
    hf, hb = lax.fori_loop(nch // 2, nch, functools.partial(step, accumulate=True), carry)
    st_ref[0] = jnp.concatenate([hf, hb], axis=0)


def _rglru_call(xr, conv_w, conv_b, wa_bd, wx_bd, ba, bx, lam, h0, batch, seq_len):
    cb = RNN_CB
    ncb = D_RNN // cb
    xr3 = xr.reshape(batch, seq_len, D_RNN)
    hs, st = pl.pallas_call(
        functools.partial(_rglru_kernel, seq_len=seq_len),
        grid=(batch, ncb),
        in_specs=[pl.BlockSpec((1, seq_len, cb), lambda b, c: (b, 0, c)),
                  pl.BlockSpec((4, cb), lambda b, c: (0, c)),
                  pl.BlockSpec((1, cb), lambda b, c: (0, c)),
                  pl.BlockSpec((2, 1, cb, cb), lambda b, c: (0, c, 0, 0)),
                  pl.BlockSpec((2, 1, cb, cb), lambda b, c: (0, c, 0, 0)),
                  pl.BlockSpec((2, cb), lambda b, c: (0, c)),
                  pl.BlockSpec((2, cb), lambda b, c: (0, c)),
                  pl.BlockSpec((2, cb), lambda b, c: (0, c)),
                  pl.BlockSpec((1, 2, cb), lambda b, c: (b, 0, c))],
        out_specs=[pl.BlockSpec((1, seq_len, cb), lambda b, c: (b, 0, c)),
                   pl.BlockSpec((1, 2, cb), lambda b, c: (b, 0, c))],
        out_shape=[jax.ShapeDtypeStruct((batch, seq_len, D_RNN), F32),
                   jax.ShapeDtypeStruct((batch, 2, D_RNN), F32)],
        scratch_shapes=[pltpu.VMEM((seq_len, cb), F32) for _ in range(4)],
        compiler_params=_cparams(("parallel", "parallel")),
        name="rglru",
    )(xr3, conv_w, conv_b, wa_bd, wx_bd, ba, bx, lam, h0)
    return hs.reshape(batch * seq_len, D_RNN), st


def _pool_kernel(xp_ref, w_ref, b_ref, sc_ref, o_ref, *, seq_len):
    L = seq_len
    row = lax.broadcasted_iota(jnp.int32, (L, POOL_GW), 0)

    def shifted(v, s):
        r = pltpu.roll(v, s % L, 0)
        return jnp.where(row >= s, r, 0.0) if s > 0 else jnp.where(row < L + s, r, 0.0)

    for g, w in enumerate(POOL_WINDOWS):
        lanes = slice(g * POOL_GW, (g + 1) * POOL_GW)
        x = xp_ref[0, :, lanes]
        half = w // 2
        acc = x
        for k in range(-half, half):
            if k != 0:
                acc = acc + shifted(x, -k)
        cnt = (jnp.minimum(row + half, L) - jnp.maximum(row - half, 0)).astype(F32)
        pooled = acc / cnt - x
        y = jnp.dot(pooled.astype(BF16), w_ref[g], preferred_element_type=F32) + b_ref[:, lanes]
        o_ref[0, :, lanes] = y * sc_ref[:, lanes]


def _pool_call(xp, pool_w_bf, pool_b, pool_scale, batch, seq_len):
    xp3 = xp.reshape(batch, seq_len, D_POOL)
    out = pl.pallas_call(
        functools.partial(_pool_kernel, seq_len=seq_len),
        grid=(batch,),
        in_specs=[pl.BlockSpec((1, seq_len, D_POOL), lambda b: (b, 0, 0)),
                  pl.BlockSpec(pool_w_bf.shape, lambda b: (0, 0, 0)),
                  pl.BlockSpec((1, D_POOL), lambda b: (0, 0)),
                  pl.BlockSpec((1, D_POOL), lambda b: (0, 0))],
        out_specs=pl.BlockSpec((1, seq_len, D_POOL), lambda b: (b, 0, 0)),
        out_shape=jax.ShapeDtypeStruct((batch, seq_len, D_POOL), F32),
        compiler_params=_cparams(("parallel",)),
        name="pool",
    )(xp3, pool_w_bf, pool_b, pool_scale)
    return out.reshape(batch * seq_len, D_POOL)


def _mixout_kernel(*refs, has_pos):
    if has_pos:
        (hs_ref, gy_ref, yp_ref, sga_ref, sgb_ref, x_ref, pos_ref, mod_ref, wur_ref, wup_ref,
         wo_ref, g_ref, x1_ref, h2_ref) = refs
        x = x_ref[...] + pos_ref[...]
    else:
        (hs_ref, gy_ref, yp_ref, sga_ref, sgb_ref, x_ref, mod_ref, wur_ref, wup_ref,
         wo_ref, g_ref, x1_ref, h2_ref) = refs
        x = x_ref[...]
    mod = mod_ref[0]
    g1 = mod[:, 2 * D_MODEL:3 * D_MODEL]
    sh2 = mod[:, 3 * D_MODEL:4 * D_MODEL]
    sc2 = mod[:, 4 * D_MODEL:5 * D_MODEL]
    branch_a = jnp.dot((hs_ref[...] * gy_ref[...]).astype(BF16), wur_ref[...],
                       preferred_element_type=F32)
    branch_b = jnp.dot(yp_ref[...].astype(BF16), wup_ref[...], preferred_element_type=F32)
    merged = sga_ref[...] * branch_a + sgb_ref[...] * branch_b
    m = jnp.dot(merged.astype(BF16), wo_ref[...], preferred_element_type=F32)
    x1 = x + g1 * m
    x1_ref[...] = x1
    h2_ref[...] = (_rmsnorm(x1, g_ref[...]) * (1.0 + sc2) + sh2).astype(BF16)


def _mixout_call(hs, gy, yp, sga, sgb, x, pos, mods, wur_bf, wup_bf, wo_bf, norm_g, seq_len):
    t = x.shape[0]
    tm = TOK_TILE
    n_tiles = t // tm
    tok, modm, posm = _tile_maps(n_tiles, mods.shape[0], seq_len // tm)
    in_specs = [pl.BlockSpec((tm, D_RNN), tok), pl.BlockSpec((tm, D_RNN), tok),
                pl.BlockSpec((tm, D_POOL), tok), pl.BlockSpec((tm, D_MODEL), tok),
                pl.BlockSpec((tm, D_MODEL), tok), pl.BlockSpec((tm, D_MODEL), tok)]
    args = [hs, gy, yp, sga, sgb, x]
    if pos is not None:
        in_specs.append(pl.BlockSpec((tm, D_MODEL), posm))
        args.append(pos)
    in_specs += [pl.BlockSpec((1, 1, mods.shape[2]), modm),
                 _const_spec((D_RNN, D_MODEL)), _const_spec((D_POOL, D_MODEL)),
                 _const_spec((D_MODEL, D_MODEL)), _const_spec((1, D_MODEL))]
    args += [mods, wur_bf, wup_bf, wo_bf, norm_g]
    return pl.pallas_call(
        functools.partial(_mixout_kernel, has_pos=pos is not None),
        grid=(n_tiles,),
        in_specs=in_specs,
        out_specs=[pl.BlockSpec((tm, D_MODEL), tok), pl.BlockSpec((tm, D_MODEL), tok)],
        out_shape=[jax.ShapeDtypeStruct((t, D_MODEL), F32),
                   jax.ShapeDtypeStruct((t, D_MODEL), BF16)],
        compiler_params=_cparams(("parallel",)),
        name="mixout",
    )(*args)


def _sort16_pairs():
    pairs = []

    def merge(lo, n, r):
        step = r * 2
        if step < n:
            merge(lo, n, step)
            merge(lo + r, n, step)
            pairs.extend((i, i + r) for i in range(lo + r, lo + n - r, step))
        else:
            pairs.append((lo, lo + r))

    def sort(lo, n):
        if n > 1:
            sort(lo, n // 2)
            sort(lo + n // 2, n // 2)
            merge(lo, n, 1)

    sort(0, PEER_TOPK)
    return pairs


_SORT16 = _sort16_pairs()


def _cmpx(v, i, j):
    hi, lo = jnp.maximum(v[i], v[j]), jnp.minimum(v[i], v[j])
    v[i], v[j] = hi, lo


def _bitonic_merge16(v):
    for d in (8, 4, 2, 1):
        for i in range(PEER_TOPK):
            if not i & d:
                _cmpx(v, i, i + d)
    return v


def _max_opt(a, b):
    if a is None or b is None:
        return b if a is None else a
    return jnp.maximum(a, b)


def _merge_top16(top, rest, other, other_rest=None):
    z = list(top)
    for k, o in enumerate(other):
        i = PEER_TOPK - 1 - k
        z[i] = jnp.maximum(top[i], o)
        rest = _max_opt(rest, jnp.minimum(top[i], o))
    return _bitonic_merge16(z), _max_opt(rest, other_rest)


def _top17_of_keys(ref, tc):
    lists = []
    for g in range(N_KEYS // PEER_TOPK):
        v = [ref[tc, pl.ds((g * PEER_TOPK + i) * SUBLANES, SUBLANES), :] for i in range(PEER_TOPK)]
        for i, j in _SORT16:
            _cmpx(v, i, j)
        lists.append((v, None))
    while len(lists) > 1:
        lists = [_merge_top16(a, ra, b, rb)
                 for (a, ra), (b, rb) in zip(lists[0::2], lists[1::2])]
    return lists[0]


def _gelu_times(x, w_half):
    c = 0.7978845608028654
    t = jnp.tanh(x * (c + (c * 0.044715) * (x * x)))
    return (x + x * t) * w_half


def _peer_route(h2t_ref, wqt_ref, kbd_ref, keys2_ref, se_ref, th_ref, f1_ref):
    tm = h2t_ref.shape[1]
    nq = N_KEYS * PEER_HEADS
    qt = jnp.dot(wqt_ref[...], h2t_ref[...], preferred_element_type=F32)
    q1, q2 = qt[0:nq].astype(BF16), qt[nq:2 * nq].astype(BF16)
    sc1 = jnp.dot(kbd_ref[0], q1, preferred_element_type=F32)
    sc2 = jnp.dot(kbd_ref[1], q2, preferred_element_type=F32)
    for tc in range(tm // LANES):
        th_ref[tc] = sc1[:, tc * LANES:(tc + 1) * LANES]
        f1_ref[tc] = sc2[:, tc * LANES:(tc + 1) * LANES]
    for h in range(PEER_HEADS):
        s2 = jnp.dot(keys2_ref[h], q2[h * N_KEYS:(h + 1) * N_KEYS],
                     preferred_element_type=F32)
        for tc in range(tm // LANES):
            se_ref[h, tc, 0] = s2[:, tc * LANES:(tc + 1) * LANES]
    for tc in range(tm // LANES):
        v1, r1 = _top17_of_keys(th_ref, tc)
        v2, r2 = _top17_of_keys(f1_ref, tc)
        top, rest = [v1[0] + v2[j] for j in range(PEER_TOPK)], None
        for i in range(1, PEER_TOPK):
            row = [v1[i] + v2[j] for j in range(PEER_TOPK // (i + 1))]
            top, rest = _merge_top16(top, rest, row)
        c1, c2 = r1 + v2[0], v1[0] + r2
        top, rest = _merge_top16(top, rest, [jnp.maximum(c1, c2), jnp.minimum(c1, c2)])
        z = jnp.exp(top[1] - top[0]) + 1.0
        for k in range(2, PEER_TOPK):
            z = z + jnp.exp(top[k] - top[0])
        half_inv_z = 0.5 / z
        tau = 0.5 * (top[PEER_TOPK - 1] + rest)
        for a in range(N_KEYS):
            rows = pl.ds(a * SUBLANES, SUBLANES)
            s1 = th_ref[tc, rows, :]
            th_ref[tc, rows, :] = tau - s1
            f1_ref[tc, rows, :] = jnp.exp(s1 - v1[0]) * half_inv_z
        for h in range(PEER_HEADS):
            se_ref[h, tc, 1] = jnp.exp(se_ref[h, tc, 0] - v2[0][h:h + 1])


PEER_REGION = 2 * LANES


def _peer_scores(u_ref, h2t_ref, st_ref, region):
    c0 = region * PEER_REGION
    s = jnp.dot(u_ref[...], h2t_ref[:, c0:c0 + PEER_REGION], preferred_element_type=F32)
    for k in range(PEER_REGION // LANES):
        st_ref[c0 // LANES + k] = s[:, k * LANES:(k + 1) * LANES]


def _peer_weighted_act(st_ref, pt_ref, se_ref, tf_ref, half, region):
    _, eb, _ = st_ref.shape
    nslab = eb // N_KEYS
    tcs = range(region * PEER_REGION // LANES, (region + 1) * PEER_REGION // LANES)
    for a in range(nslab):
        rows8 = slice((half * nslab + a) * SUBLANES, (half * nslab + a + 1) * SUBLANES)
        rows = slice(a * N_KEYS, (a + 1) * N_KEYS)
        for tc in tcs:
            th8, f18 = tf_ref[tc, 0, rows8, :], tf_ref[tc, 1, rows8, :]
            w = None
            for h in range(PEER_HEADS):
                term = jnp.where(se_ref[h, tc, 0] >= th8[h:h + 1],
                                 se_ref[h, tc, 1] * f18[h:h + 1], 0.0)
                w = term if w is None else w + term
            pt_ref[rows, tc * LANES:(tc + 1) * LANES] = _gelu_times(st_ref[tc, rows, :], w).astype(BF16)


def _peer_kernel(h2_ref, x1_ref, mod_ref, wqt_ref, kbd_ref, keys2_ref, u_first_ref, u_b_ref, u_an_ref,
                 vt_a_ref, vt_b_ref, g_ref, y_ref,
                 se_ref, th_ref, f1_ref, acc_ref, h2t_ref, tf_ref, st_ref, pt_ref):
    j = pl.program_id(1)
    ntc = st_ref.shape[0]
    nreg = ntc * LANES // PEER_REGION

    @pl.when(j == 0)
    def _():
        h2t_ref[...] = h2_ref[...].T
        _peer_route(h2t_ref, wqt_ref, kbd_ref, keys2_ref, se_ref, th_ref, f1_ref)
        acc_ref[...] = jnp.zeros_like(acc_ref)
        for r in range(nreg):
            _peer_scores(u_first_ref, h2t_ref, st_ref, r)

    nrow = tf_ref.shape[2]
    step_rows = pl.ds(pl.multiple_of(j * nrow, nrow), nrow)
    for tc in range(ntc):
        tf_ref[tc, 0] = th_ref[tc, step_rows, :]
        tf_ref[tc, 1] = f1_ref[tc, step_rows, :]

    for half, (vt_ref, u_next_ref) in enumerate(((vt_a_ref, u_b_ref), (vt_b_ref, u_an_ref))):
        for r in range(nreg):
            cols = slice(r * PEER_REGION, (r + 1) * PEER_REGION)
            _peer_weighted_act(st_ref, pt_ref, se_ref, tf_ref, half, r)
            acc_ref[:, cols] += jnp.dot(vt_ref[...], pt_ref[:, cols], preferred_element_type=F32)
            _peer_scores(u_next_ref, h2t_ref, st_ref, r)

    @pl.when(j == pl.num_programs(1) - 1)
    def _():
        g2 = mod_ref[0][:, 5 * D_MODEL:6 * D_MODEL]
        xo = x1_ref[...] + g2 * acc_ref[...].T
        y_ref[...] = _rmsnorm(xo, g_ref[...])


def _peer_call(h2, x1, mods, wqt_bf, kbd_bf, keys2_bf, u_bf, vt_bf, final_g, seq_len):
    t = h2.shape[0]
    tm, eb = PEER_TM, PEER_EB
    n_tiles = t // tm
    tiles_per_mod = n_tiles // mods.shape[0]
    nb = N_EXPERTS // eb
    tok = lambda i, j: (i, 0)
    nq = N_KEYS * PEER_HEADS
    ntc = tm // LANES
    once = dict(pipeline_mode=pl.Buffered(1))
    return pl.pallas_call(
        _peer_kernel,
        grid=(n_tiles, nb // 2),
        in_specs=[pl.BlockSpec((tm, D_MODEL), tok),
                  pl.BlockSpec((tm, D_MODEL), tok, **once),
                  pl.BlockSpec((1, 1, mods.shape[2]), lambda i, j: (i // tiles_per_mod, 0, 0)),
                  pl.BlockSpec(wqt_bf.shape, lambda i, j: (0, 0), **once),
                  pl.BlockSpec(kbd_bf.shape, lambda i, j: (0, 0, 0), **once),
                  pl.BlockSpec(keys2_bf.shape, lambda i, j: (0, 0, 0), **once),
                  pl.BlockSpec((eb, D_MODEL), lambda i, j: (0, 0), **once),
                  pl.BlockSpec((eb, D_MODEL), lambda i, j: (2 * j + 1, 0)),
                  pl.BlockSpec((eb, D_MODEL), lambda i, j: (jnp.minimum(2 * j + 2, nb - 1), 0)),
                  pl.BlockSpec((D_MODEL, eb), lambda i, j: (0, 2 * j)),
                  pl.BlockSpec((D_MODEL, eb), lambda i, j: (0, 2 * j + 1)),
                  pl.BlockSpec((1, D_MODEL), lambda i, j: (0, 0), **once)],
        out_specs=pl.BlockSpec((tm, D_MODEL), tok),
        out_shape=jax.ShapeDtypeStruct((t, D_MODEL), F32),
        scratch_shapes=[pltpu.VMEM((PEER_HEADS, ntc, 2, N_KEYS, LANES), F32),
                        pltpu.VMEM((ntc, nq, LANES), F32),
                        pltpu.VMEM((ntc, nq, LANES), F32),
                        pltpu.VMEM((D_MODEL, tm), F32),
                        pltpu.VMEM((D_MODEL, tm), BF16),
                        pltpu.VMEM((ntc, 2, 2 * (eb // N_KEYS) * SUBLANES, LANES), F32),
                        pltpu.VMEM((ntc, eb, LANES), F32),
                        pltpu.VMEM((eb, tm), BF16)],
        compiler_params=_cparams(("parallel", "arbitrary"), PEER_SCHED_FLAGS),
        name="peer",
    )(h2, x1, mods, wqt_bf, kbd_bf, keys2_bf, u_bf, u_bf, u_bf, vt_bf, vt_bf, final_g)


def _grid_pos_embed(n_tokens):
    rows = n_tokens // GRID_W
    nq = D_MODEL // 4
    omega = 1.0 / (POS_BASE ** (jnp.arange(nq, dtype=F32) / nq))
    r = jnp.broadcast_to(jnp.arange(rows, dtype=F32)[:, None], (rows, GRID_W)).reshape(-1)
    col = jnp.broadcast_to(jnp.arange(GRID_W, dtype=F32)[None, :], (rows, GRID_W)).reshape(-1)
    er = r[:, None] * omega[None, :]
    ec = col[:, None] * omega[None, :]
    return jnp.concatenate([jnp.sin(er), jnp.cos(er), jnp.sin(ec), jnp.cos(ec)], axis=-1)


def _block_diag(w):
    per = RNN_CB // RNN_HB
    w5 = w.reshape(2, D_RNN // RNN_CB, per, RNN_HB, RNN_HB)
    eye = jnp.eye(per, dtype=w.dtype)
    bd = jnp.einsum("dcipq,ij->dcipjq", w5, eye)
    return bd.reshape(2, D_RNN // RNN_CB, RNN_CB, RNN_CB).astype(BF16)


def kernel(x_prompt, x_sample, state_rglru, c, c_ctx, norm1_g, norm2_g, w_mod, b_mod, w_in, conv_w,
           conv_b, rg_wa, rg_ba, rg_wx, rg_bx, rg_lambda, w_up_rnn, pool_w, pool_b, pool_scale,
           w_up_pool, w_o, peer_wq, peer_keys, peer_u, peer_v, final_g):
    assert w_mod.shape[0] == 1, "single trunk layer"
    l = 0
    batch, seq = x_prompt.shape[0], x_prompt.shape[1]
    dec_batch, dec_seq = x_sample.shape[0], x_sample.shape[1]

    n_c = 1 + dec_batch
    cc = jnp.zeros((2 * SUBLANES, D_MODEL), F32).at[0].set(c_ctx).at[1:n_c].set(c)
    mods = _mod_call(cc, w_mod[l], b_mod[l][None])
    mods_ctx = mods[0:1][:, None, :]
    mods_lat = mods[1:n_c][:, None, :]

    w_in_bf = w_in[l].astype(BF16)
    wa_bd, wx_bd = _block_diag(rg_wa[l]), _block_diag(rg_wx[l])
    pool_w_bf = pool_w[l].astype(BF16)
    wur_bf, wup_bf, wo_bf = w_up_rnn[l].astype(BF16), w_up_pool[l].astype(BF16), w_o[l].astype(BF16)
    wqt_bf = (peer_wq[l].T.reshape(PEER_HEADS, 2, N_KEYS, D_MODEL).transpose(1, 0, 2, 3)
              .reshape(2 * PEER_HEADS * N_KEYS, D_MODEL).astype(BF16))
    kbd_bf = (jnp.einsum("hpkd,hg->pkhgd", peer_keys[l], jnp.eye(PEER_HEADS, dtype=F32))
              .reshape(2, N_KEYS * PEER_HEADS, PEER_HEADS * N_KEYS).astype(BF16))
    keys2_bf = peer_keys[l][:, 1].astype(BF16)
    u_bf = peer_u[l].astype(BF16)
    vt_bf = peer_v[l].T.astype(BF16)
    n1, n2, fg = norm1_g[l][None], norm2_g[l][None], final_g[None]
    pos = _grid_pos_embed(dec_seq)

    def stream(x, pos_emb, mods_s, h0, b, seq_len):
        xr, gy, xp, sga, sgb = _inproj_call(x, pos_emb, mods_s, n1, w_in_bf, seq_len)
        hs, st = _rglru_call(xr, conv_w[l], conv_b[l][None], wa_bd, wx_bd, rg_ba[l], rg_bx[l],
                             rg_lambda[l], h0, b, seq_len)
        yp = _pool_call(xp, pool_w_bf, pool_b[l][None], pool_scale[l][None], b, seq_len)
        x1, h2 = _mixout_call(hs, gy, yp, sga, sgb, x, pos_emb, mods_s, wur_bf, wup_bf, wo_bf, n2,
                              seq_len)
        y = _peer_call(h2, x1, mods_s, wqt_bf, kbd_bf, keys2_bf, u_bf, vt_bf, fg, seq_len)
        return y.reshape(b, seq_len, D_MODEL), st

    h0_ctx = jnp.zeros((batch, 2, D_RNN), F32)
    y_prompt, st_ctx = stream(x_prompt.reshape(batch * seq, D_MODEL), None, mods_ctx, h0_ctx,
                              batch, seq)
    y_sample, _ = stream(x_sample.reshape(dec_batch * dec_seq, D_MODEL), pos, mods_lat,
                         state_rglru[:, l], dec_batch, dec_seq)
    return (y_prompt, y_sample, st_ctx[:, None])
```
